```python
import math
import jax, jax.numpy as jnp
from jax import lax
import numpy as np

D_MODEL = 1024
BATCH = 32
SEQ = 256
DEPTH = 2
DEC_BATCH = 8
DEC_SEQ = 2048
PAST_LEN = 512

GRID_W = 64
HEAD_DIM = 64
ROPE_FREQS = HEAD_DIM // 4
ROPE_THETA = 10000.0
Q_BLOCK = 128
EPS = 1e-6
A_HEADS = D_MODEL // (2 * HEAD_DIM)
A_KV_HEADS = A_HEADS // 4
A_Q_W = A_HEADS * HEAD_DIM
A_KV_W = A_KV_HEADS * HEAD_DIM
LRU_WIDTH = D_MODEL // 2
LRU_BLOCKS = 8
LRU_BLOCK_W = LRU_WIDTH // LRU_BLOCKS
CONV_W = 4
CONV_LEFT = 1
RG_C = 8.0
HYB_SPLITS = (A_Q_W, A_Q_W + A_KV_W, A_Q_W + 2 * A_KV_W, A_Q_W + 2 * A_KV_W + LRU_WIDTH)
HYB_IN_W = A_Q_W + 2 * A_KV_W + 2 * LRU_WIDTH
C_HEADS = D_MODEL // (2 * HEAD_DIM)
C_IN_W = 3 * 2 * C_HEADS * HEAD_DIM
N_EXPERTS = 16
EXPERT_FF = D_MODEL
EC_FACTOR = 2
ADA_CHUNKS = 6

kernel_name = 'hybrid_diffusion_prefix_trunk_step'


def rms_norm(x, g):
    xf = x.astype(jnp.float32)
    y = xf * lax.rsqrt(jnp.mean(xf * xf, axis=-1, keepdims=True) + EPS)
    return (y * g.astype(jnp.float32)).astype(x.dtype)


def adaln(cond, w, b):
    m = jax.nn.silu(cond) @ w + b
    return [t[:, None, :] for t in jnp.split(m, ADA_CHUNKS, axis=-1)]


def modulate(h, shift, scale):
    return h * (1 + scale) + shift


def axial_rope_tables(n_tokens):
    rows = n_tokens // GRID_W
    pos_r = jnp.repeat(jnp.arange(rows, dtype=jnp.float32), GRID_W)
    pos_c = jnp.tile(jnp.arange(GRID_W, dtype=jnp.float32), rows)
    inv = ROPE_THETA ** (-jnp.arange(ROPE_FREQS, dtype=jnp.float32) / ROPE_FREQS)
    ang = jnp.stack([pos_r[:, None] * inv, pos_c[:, None] * inv], axis=1)
    return jnp.cos(ang), jnp.sin(ang)


def apply_axial_rope(x, cos, sin):
    L = x.shape[1]
    bshape = (1, L) + (1,) * (x.ndim - 3) + (2, ROPE_FREQS)
    cs = cos.reshape(bshape).astype(x.dtype)
    sn = sin.reshape(bshape).astype(x.dtype)
    xr = x.reshape(x.shape[:-1] + (2, 2, ROPE_FREQS))
    x1, x2 = xr[..., 0, :], xr[..., 1, :]
    out = jnp.stack([x1 * cs - x2 * sn, x2 * cs + x1 * sn], axis=-2)
    return out.reshape(x.shape)


def blocked_gqa(q, k, v):
    B, Lq, H, D = q.shape
    kvh = k.shape[2]
    grp = H // kvh
    nb = Lq // Q_BLOCK
    qb = q.reshape(B, nb, Q_BLOCK, kvh, grp, D).transpose(1, 0, 2, 3, 4, 5)
    scale = D ** -0.5

    def one_block(qblk):
        s = jnp.einsum('bqkgd,bskd->bkgqs', qblk, k, preferred_element_type=jnp.float32) * scale
        pr = jax.nn.softmax(s, axis=-1).astype(v.dtype)
        return jnp.einsum('bkgqs,bskd->bqkgd', pr, v)

    o = lax.map(one_block, qb)
    return o.transpose(1, 0, 2, 3, 4, 5).reshape(B, Lq, H * D)


def blocked_diff_attention(q, k, v, lam):
    B, Lq, _, H, D = q.shape
    nb = Lq // Q_BLOCK
    qb = q.reshape(B, nb, Q_BLOCK, 2, H, D).transpose(1, 0, 2, 3, 4, 5)
    scale = D ** -0.5

    def one_block(qblk):
        s = jnp.einsum('bqphd,bsphd->bphqs', qblk, k, preferred_element_type=jnp.float32) * scale
        pr = jax.nn.softmax(s, axis=-1)
        w = (pr[:, 0] - lam * pr[:, 1]).astype(v.dtype)
        return jnp.einsum('bhqs,bshe->bqhe', w, v)

    o = lax.map(one_block, qb)
    return o.transpose(1, 0, 2, 3, 4).reshape(B, Lq, H, 2 * D)


def centred_dwconv(x, w, b):
    L = x.shape[1]
    xp = jnp.pad(x, ((0, 0), (CONV_LEFT, CONV_W - 1 - CONV_LEFT), (0, 0)))
    return b + sum(xp[:, j:j + L] * w[j] for j in range(CONV_W))


def _lin_combine(e1, e2):
    a1, b1 = e1
    a2, b2 = e2
    return a1 * a2, a2 * b1 + b2


def rglru_scan(x, prm, h0, reverse):
    wa, ba, wi, bi, lam = prm
    B, L, W = x.shape
    f32 = jnp.float32
    xf = x.astype(f32)
    xblk = xf.reshape(B, L, LRU_BLOCKS, LRU_BLOCK_W)
    r = jax.nn.sigmoid(jnp.einsum('blnd,nde->blne', xblk, wa.astype(f32)).reshape(B, L, W) + ba.astype(f32))
    i = jax.nn.sigmoid(jnp.einsum('blnd,nde->blne', xblk, wi.astype(f32)).reshape(B, L, W) + bi.astype(f32))
    log_a = -RG_C * r * jax.nn.softplus(-lam.astype(f32))
    a = jnp.exp(log_a)
    b = jnp.sqrt(-jnp.expm1(2.0 * log_a)) * (i * xf)
    if reverse:
        a, b = jnp.flip(a, axis=1), jnp.flip(b, axis=1)
    acc_a, acc_b = lax.associative_scan(_lin_combine, (a, b), axis=1)
    hs = acc_a * h0.astype(f32)[:, None, :] + acc_b
    if reverse:
        hs = jnp.flip(hs, axis=1)
    return hs


def hybrid_mixer(h, p, ctx, rope):
    B, L, _ = h.shape
    q, k, v, xb, gb = jnp.split(h @ p['w_in'], HYB_SPLITS, axis=-1)
    q = rms_norm(q.reshape(B, L, A_HEADS, HEAD_DIM), p['q_gain'])
    k = rms_norm(k.reshape(B, L, A_KV_HEADS, HEAD_DIM), p['k_gain'])
    v = v.reshape(B, L, A_KV_HEADS, HEAD_DIM)
    xc = centred_dwconv(xb, p['conv_w'], p['conv_b'])
    if ctx is None:
        kk, vv = k, v
        h0f = jnp.zeros((B, LRU_WIDTH), jnp.float32)
        h0b = h0f
    else:
        ck, cv, h0f, h0b = ctx
        q = apply_axial_rope(q, *rope)
        k = apply_axial_rope(k, *rope)
        kk = jnp.concatenate([ck, k], axis=1)
        vv = jnp.concatenate([cv, v], axis=1)
    attn = blocked_gqa(q, kk, vv)
    hf = rglru_scan(xc, p['fwd'], h0f, reverse=False)
    hb = rglru_scan(xc, p['bwd'], h0b, reverse=True)
    rec = (hf + hb).astype(h.dtype) * jax.nn.gelu(gb)
    out = jnp.concatenate([attn, rec], axis=-1) @ p['w_out']
    if ctx is None:
        state = (k, v, hf[:, -1].astype(h.dtype), hb[:, 0].astype(h.dtype))
    else:
        state = None
    return out, state


def diff_mixer(h, p, ctx, rope):
    B, L, _ = h.shape
    f32 = jnp.float32
    q, k, v = jnp.split(h @ p['w_in'], 3, axis=-1)
    q = rms_norm(q.reshape(B, L, 2, C_HEADS, HEAD_DIM), p['q_gain'])
    k = rms_norm(k.reshape(B, L, 2, C_HEADS, HEAD_DIM), p['k_gain'])
    v = v.reshape(B, L, C_HEADS, 2 * HEAD_DIM)
    lam = (jnp.exp(jnp.sum(p['lam_q1'].astype(f32) * p['lam_k1'].astype(f32)))
           - jnp.exp(jnp.sum(p['lam_q2'].astype(f32) * p['lam_k2'].astype(f32)))
           + p['lam_init'])
    if ctx is None:
        kk, vv = k, v
        state = (k, v)
    else:
        ck, cv = ctx
        q = apply_axial_rope(q, *rope)
        k = apply_axial_rope(k, *rope)
        kk = jnp.concatenate([ck, k], axis=1)
        vv = jnp.concatenate([cv, v], axis=1)
        state = None
    o = blocked_diff_attention(q, kk, vv, lam)
    o = rms_norm(o, p['subln']) * (1.0 - p['lam_init'])
    return o.reshape(B, L, C_HEADS * 2 * HEAD_DIM) @ p['w_out'], state


def ec_moe(h, p):
    B, L, D = h.shape
    n = B * L
    cap = EC_FACTOR * n // N_EXPERTS
    xt = h.reshape(n, D)
    aff = jax.nn.softmax(jnp.matmul(xt, p['router'], preferred_element_type=jnp.float32), axis=-1)
    gate, idx = lax.top_k(aff.T, cap)
    xe = jnp.take(xt, idx, axis=0)
    hid = jax.nn.silu(jnp.einsum('ecd,edf->ecf', xe, p['w_gate'])) * jnp.einsum('ecd,edf->ecf', xe, p['w_up'])
    ye = jnp.einsum('ecf,efd->ecd', hid, p['w_down']) * gate[..., None].astype(h.dtype)
    out = jnp.zeros_like(xt).at[idx.reshape(-1)].add(ye.reshape(-1, D))
    return out.reshape(B, L, D)


def trunk_layer(x, cond, p, mixer, ctx, rope):
    sh1, sc1, g1, sh2, sc2, g2 = adaln(cond, p['ada_w'], p['ada_b'])
    mix, state = mixer(modulate(rms_norm(x, p['norm1']), sh1, sc1), p, ctx, rope)
    x = x + g1 * mix
    x = x + g2 * ec_moe(modulate(rms_norm(x, p['norm2']), sh2, sc2), p)
    return x, state


def diff_lambda_init(layer_idx):
    return 0.8 - 0.6 * math.exp(-0.3 * layer_idx)


def setup_inputs(seed: int = 0) -> dict:
    key = jax.random.key(seed)
    keys = jax.random.split(key, 128)
    ctr = [0]

    def nk():
        ctr[0] += 1
        return keys[ctr[0]]

    def nrm(shape, scale=1.0):
        return jax.random.normal(nk(), shape, jnp.float32) * scale

    def gain(n):
        return 1.0 + nrm((n,), 0.02)

    def lru_lambda():
        u = jax.random.uniform(nk(), (LRU_WIDTH,), jnp.float32, 0.9, 0.999)
        a0 = u ** (1.0 / RG_C)
        return jnp.log(a0) - jnp.log1p(-a0)

    d = D_MODEL
    inp = {}
    inp['x_prompt'] = nrm((BATCH, SEQ, d))
    inp['x_sample'] = nrm((DEC_BATCH, DEC_SEQ, d))
    inp['cache_l0_k'] = nrm((DEC_BATCH, PAST_LEN, A_KV_HEADS, HEAD_DIM))
    inp['cache_l0_v'] = nrm((DEC_BATCH, PAST_LEN, A_KV_HEADS, HEAD_DIM))
    inp['state_l0_fwd'] = nrm((DEC_BATCH, LRU_WIDTH), 0.5)
    inp['state_l0_bwd'] = nrm((DEC_BATCH, LRU_WIDTH), 0.5)
    inp['cache_l1_k'] = nrm((DEC_BATCH, PAST_LEN, 2, C_HEADS, HEAD_DIM))
    inp['cache_l1_v'] = nrm((DEC_BATCH, PAST_LEN, C_HEADS, 2 * HEAD_DIM))
    inp['c'] = nrm((DEC_BATCH, d))
    inp['c_ctx'] = nrm((d,))
    for l in range(DEPTH):
        pre = 'l%d_' % l
        inp[pre + 'norm1'] = gain(d)
        inp[pre + 'norm2'] = gain(d)
        inp[pre + 'ada_w'] = nrm((d, ADA_CHUNKS * d), 0.5 * d ** -0.5)
        inp[pre + 'ada_b'] = nrm((ADA_CHUNKS * d,), 0.02)
        if l % 2 == 0:
            inp[pre + 'w_in'] = nrm((d, HYB_IN_W), d ** -0.5)
            inp[pre + 'q_gain'] = gain(HEAD_DIM)
            inp[pre + 'k_gain'] = gain(HEAD_DIM)
            inp[pre + 'conv_w'] = nrm((CONV_W, LRU_WIDTH), CONV_W ** -0.5)
            inp[pre + 'conv_b'] = nrm((LRU_WIDTH,), 0.01)
            for dr in ('fwd', 'bwd'):
                inp[pre + dr + '_wa'] = nrm((LRU_BLOCKS, LRU_BLOCK_W, LRU_BLOCK_W), LRU_BLOCK_W ** -0.5)
                inp[pre + dr + '_ba'] = nrm((LRU_WIDTH,), 0.01)
                inp[pre + dr + '_wi'] = nrm((LRU_BLOCKS, LRU_BLOCK_W, LRU_BLOCK_W), LRU_BLOCK_W ** -0.5)
                inp[pre + dr + '_bi'] = nrm((LRU_WIDTH,), 0.01)
                inp[pre + dr + '_lam'] = lru_lambda()
            inp[pre + 'w_out'] = nrm((A_Q_W + LRU_WIDTH, d), (A_Q_W + LRU_WIDTH) ** -0.5)
        else:
            inp[pre + 'w_in'] = nrm((d, C_IN_W), d ** -0.5)
            inp[pre + 'q_gain'] = gain(HEAD_DIM)
            inp[pre + 'k_gain'] = gain(HEAD_DIM)
            inp[pre + 'lam_q1'] = nrm((HEAD_DIM,), 0.1)
            inp[pre + 'lam_k1'] = nrm((HEAD_DIM,), 0.1)
            inp[pre + 'lam_q2'] = nrm((HEAD_DIM,), 0.1)
            inp[pre + 'lam_k2'] = nrm((HEAD_DIM,), 0.1)
            inp[pre + 'subln'] = gain(2 * HEAD_DIM)
            inp[pre + 'w_out'] = nrm((2 * C_HEADS * HEAD_DIM, d), (2 * C_HEADS * HEAD_DIM) ** -0.5)
        inp[pre + 'router'] = nrm((d, N_EXPERTS), d ** -0.5)
        inp[pre + 'w_gate'] = nrm((N_EXPERTS, d, EXPERT_FF), d ** -0.5)
        inp[pre + 'w_up'] = nrm((N_EXPERTS, d, EXPERT_FF), d ** -0.5)
        inp[pre + 'w_down'] = nrm((N_EXPERTS, EXPERT_FF, d), EXPERT_FF ** -0.5)
    return inp


def reference(x_prompt, x_sample, cache_l0_k, cache_l0_v, state_l0_fwd, state_l0_bwd,
              cache_l1_k, cache_l1_v, c, c_ctx,
              l0_norm1, l0_norm2, l0_ada_w, l0_ada_b, l0_w_in, l0_q_gain, l0_k_gain,
              l0_conv_w, l0_conv_b,
              l0_fwd_wa, l0_fwd_ba, l0_fwd_wi, l0_fwd_bi, l0_fwd_lam,
              l0_bwd_wa, l0_bwd_ba, l0_bwd_wi, l0_bwd_bi, l0_bwd_lam,
              l0_w_out, l0_router, l0_w_gate, l0_w_up, l0_w_down,
              l1_norm1, l1_norm2, l1_ada_w, l1_ada_b, l1_w_in, l1_q_gain, l1_k_gain,
              l1_lam_q1, l1_lam_k1, l1_lam_q2, l1_lam_k2, l1_subln, l1_w_out,
              l1_router, l1_w_gate, l1_w_up, l1_w_down):
    layers = (
        dict(norm1=l0_norm1, norm2=l0_norm2, ada_w=l0_ada_w, ada_b=l0_ada_b,
             w_in=l0_w_in, q_gain=l0_q_gain, k_gain=l0_k_gain,
             conv_w=l0_conv_w, conv_b=l0_conv_b,
             fwd=(l0_fwd_wa, l0_fwd_ba, l0_fwd_wi, l0_fwd_bi, l0_fwd_lam),
             bwd=(l0_bwd_wa, l0_bwd_ba, l0_bwd_wi, l0_bwd_bi, l0_bwd_lam),
             w_out=l0_w_out, router=l0_router, w_gate=l0_w_gate, w_up=l0_w_up, w_down=l0_w_down),
        dict(norm1=l1_norm1, norm2=l1_norm2, ada_w=l1_ada_w, ada_b=l1_ada_b,
             w_in=l1_w_in, q_gain=l1_q_gain, k_gain=l1_k_gain,
             lam_q1=l1_lam_q1, lam_k1=l1_lam_k1, lam_q2=l1_lam_q2, lam_k2=l1_lam_k2,
             lam_init=diff_lambda_init(1), subln=l1_subln,
             w_out=l1_w_out, router=l1_router, w_gate=l1_w_gate, w_up=l1_w_up, w_down=l1_w_down),
    )
    caches = ((cache_l0_k, cache_l0_v, state_l0_fwd, state_l0_bwd), (cache_l1_k, cache_l1_v))
    mixers = (hybrid_mixer, diff_mixer)
    ctx_cond = c_ctx[None, :]
    rope = axial_rope_tables(x_sample.shape[1])
    y_p, y_s = x_prompt, x_sample
    new_state = []
    for layer in range(DEPTH):
        p = layers[layer]
        mixer = mixers[layer % 2]
        y_p, st = trunk_layer(y_p, ctx_cond, p, mixer, None, None)
        y_s, _ = trunk_layer(y_s, c, p, mixer, caches[layer], rope)
        new_state.append(st)
    (k0, v0, hf0, hb0), (k1, v1) = new_state
    return (y_p, y_s, k0, v0, hf0, hb0, k1, v1)
```

```python
import functools
import math

import jax
import jax.numpy as jnp
from jax import lax
from jax.experimental import pallas as pl
from jax.experimental.pallas import tpu as pltpu

F32 = jnp.float32
BF16 = jnp.bfloat16

D_MODEL = 1024
HEAD_DIM = 64
GRID_W = 64
ROPE_FREQS = HEAD_DIM // 4
ROPE_THETA = 10000.0
EPS = 1e-6
A_HEADS = 8
A_KV_HEADS = 2
A_Q_W = A_HEADS * HEAD_DIM
A_KV_W = A_KV_HEADS * HEAD_DIM
LRU_WIDTH = 512
LRU_BLOCKS = 8
LRU_BLOCK_W = LRU_WIDTH // LRU_BLOCKS
RG_C = 8.0
C_HEADS = 8
C_QK_W = 2 * C_HEADS * HEAD_DIM
N_EXPERTS = 16
EC_FACTOR = 2
ADA_CHUNKS = 6
LAM_INIT = 0.8 - 0.6 * math.exp(-0.3)

V7X_LANES = 128
V7X_SUBLANES = 8
V7X_MXU_DIM = 256
V7X_VMEM_LIMIT_BYTES = 56 * 1024 * 1024

COND_ROWS = 16
ROW_TILE = 512
Q_TILE = 256
FFN_TILE = 512
SCAN_CHUNK = 256
ROUTER_PAD = V7X_LANES


def _params(*sem):
    return pltpu.CompilerParams(dimension_semantics=sem, vmem_limit_bytes=V7X_VMEM_LIMIT_BYTES)


def _dot(a, b):
    return jnp.dot(a, b, preferred_element_type=F32)


def _dot_nt(a, b):
    return lax.dot_general(a, b, (((1,), (1,)), ((), ())), preferred_element_type=F32)


def _sigmoid(x):
    return 1.0 / (1.0 + jnp.exp(-x))


def _adaln_kernel(c_ref, w_ref, b_ref, o_ref):
    c = c_ref[...]
    s = (c * _sigmoid(c)).astype(BF16)
    o_ref[...] = _dot(s, w_ref[...].astype(BF16)) + b_ref[...]


def _adaln(cond, w, b):
    d, n = w.shape
    tn = 1024
    return pl.pallas_call(
        _adaln_kernel,
        grid=(n // tn,),
        in_specs=[pl.BlockSpec((COND_ROWS, d), lambda j: (0, 0)),
                  pl.BlockSpec((d, tn), lambda j: (0, j)),
                  pl.BlockSpec((1, tn), lambda j: (0, j))],
        out_specs=pl.BlockSpec((COND_ROWS, tn), lambda j: (0, j)),
        out_shape=jax.ShapeDtypeStruct((COND_ROWS, n), F32),
        name="adaln",
        compiler_params=_params("parallel"),
    )(cond, w, b.reshape(1, n))


def _rms_mod(x, nw, shift, scale):
    ms = jnp.mean(x * x, axis=-1, keepdims=True)
    return (x * lax.rsqrt(ms + EPS) * nw) * (1.0 + scale) + shift


def _head_norm(y, g_ref, gain):
    width = y.shape[1]
    outs = []
    for c0 in range(0, width, V7X_MXU_DIM):
        w = min(V7X_MXU_DIM, width - c0)
        ys = y[:, c0:c0 + w]
        sq = ys * ys
        hi = sq.astype(BF16)
        lo = (sq - hi.astype(F32)).astype(BF16)
        g = g_ref[0:w, 0:w]
        ms = (_dot(hi, g) + _dot(lo, g)) * (1.0 / HEAD_DIM)
        outs.append(ys * lax.rsqrt(ms + EPS))
    out = outs[0] if len(outs) == 1 else jnp.concatenate(outs, axis=1)
    return out * gain


def _rope(x, cos, sin):
    outs = []
    lane = lax.broadcasted_iota(jnp.int32, (x.shape[0], V7X_LANES), 1)
    first = (lane % (2 * ROPE_FREQS)) < ROPE_FREQS
    for c0 in range(0, x.shape[1], V7X_LANES):
        xs = x[:, c0:c0 + V7X_LANES]
        up = pltpu.roll(xs, V7X_LANES - ROPE_FREQS, 1)
        dn = pltpu.roll(xs, ROPE_FREQS, 1)
        outs.append(xs * cos + jnp.where(first, up, dn) * sin)
    return outs[0] if len(outs) == 1 else jnp.concatenate(outs, axis=1)


def _cond_index(i, n_prompt_tiles, tiles_per_sample):
    return jnp.where(i < n_prompt_tiles, 0, 1 + (i - n_prompt_tiles) // tiles_per_sample)


def _rope_index(i, n_prompt_tiles, tiles_per_sample):
    return jnp.where(i < n_prompt_tiles, tiles_per_sample, (i - n_prompt_tiles) % tiles_per_sample)


def _proj0_kernel(npt, xp_ref, xs_ref, mod_ref, nw_ref, w_ref, g_ref, qg_ref, kg_ref, cos_ref, sin_ref,
                  q_ref, k_ref, v_ref, xb_ref, gb_ref, kp_ref, vp_ref):
    i = pl.program_id(0)
    x = jnp.where(i < npt, xp_ref[...], xs_ref[...])
    mod = mod_ref[...]
    d = D_MODEL
    h = _rms_mod(x, nw_ref[...], mod[:, 0:d], mod[:, d:2 * d]).astype(BF16)
    cos = cos_ref[...]
    sin = sin_ref[...]
    o1, o2, o3, o4 = A_Q_W, A_Q_W + A_KV_W, A_Q_W + 2 * A_KV_W, A_Q_W + 2 * A_KV_W + LRU_WIDTH
    q = _head_norm(_dot(h, w_ref[:, 0:o1]), g_ref, qg_ref[...])
    q_ref[...] = (_rope(q, cos, sin) * (HEAD_DIM ** -0.5)).astype(BF16)
    kn = _head_norm(_dot(h, w_ref[:, o1:o2]), g_ref, kg_ref[...])
    k = _rope(kn, cos, sin)
    k_ref[...] = k.astype(BF16)
    v = _dot(h, w_ref[:, o2:o3])
    v_ref[...] = v.astype(BF16)
    xb_ref[...] = _dot(h, w_ref[:, o3:o4])
    gb_ref[...] = _dot(h, w_ref[:, o4:])

    @pl.when(i < npt)
    def _():
        kp_ref[...] = k
        vp_ref[...] = v


def _proj1_kernel(npt, x1_ref, moe_ref, modp_ref, mod_ref, nw_ref, w_ref, g_ref, qg_ref, kg_ref, cos_ref, sin_ref,
                  x2_ref, q_ref, k_ref, v_ref, kp_ref, vp_ref):
    i = pl.program_id(0)
    d = D_MODEL
    x = x1_ref[...] + modp_ref[:, 5 * d:6 * d] * moe_ref[...]
    x2_ref[...] = x
    mod = mod_ref[...]
    h = _rms_mod(x, nw_ref[...], mod[:, 0:d], mod[:, d:2 * d]).astype(BF16)
    cos = cos_ref[...]
    sin = sin_ref[...]
    w = C_QK_W
    q = _head_norm(_dot(h, w_ref[:, 0:w]), g_ref, qg_ref[...])
    q_ref[...] = (_rope(q, cos, sin) * (HEAD_DIM ** -0.5)).astype(BF16)
    kn = _head_norm(_dot(h, w_ref[:, w:2 * w]), g_ref, kg_ref[...])
    k = _rope(kn, cos, sin)
    k_ref[...] = k.astype(BF16)
    v = _dot(h, w_ref[:, 2 * w:3 * w])
    v_ref[...] = v.astype(BF16)

    @pl.when(i < npt)
    def _():
        kp_ref[...] = k
        vp_ref[...] = v


def _group_matrix():
    idx = jnp.arange(V7X_MXU_DIM) // HEAD_DIM
    return (idx[:, None] == idx[None, :]).astype(BF16)


def _rope_tables(seq, tile):
    rows = seq // GRID_W
    pos_r = jnp.repeat(jnp.arange(rows, dtype=F32), GRID_W)
    pos_c = jnp.tile(jnp.arange(GRID_W, dtype=F32), rows)
    inv = ROPE_THETA ** (-jnp.arange(ROPE_FREQS, dtype=F32) / ROPE_FREQS)
    ang_r = pos_r[:, None] * inv
    ang_c = pos_c[:, None] * inv
    cos = jnp.concatenate([jnp.cos(ang_r)] * 2 + [jnp.cos(ang_c)] * 2, axis=1)
    sin = jnp.concatenate([-jnp.sin(ang_r), jnp.sin(ang_r), -jnp.sin(ang_c), jnp.sin(ang_c)], axis=1)
    reps = V7X_LANES // HEAD_DIM
    cos = jnp.concatenate([jnp.tile(cos, (1, reps)), jnp.ones((tile, V7X_LANES), F32)], axis=0)
    sin = jnp.concatenate([jnp.tile(sin, (1, reps)), jnp.zeros((tile, V7X_LANES), F32)], axis=0)
    return cos, sin


def _row_spec(tile, width):
    return pl.BlockSpec((tile, width), lambda i: (i, 0))


def _const_spec(shape):
    return pl.BlockSpec(shape, lambda i: (0,) * len(shape))


def _proj0(xp, xs, seq_s, mod, nw, w_in, gmat, qg, kg, cos, sin):
    n_p, n_s = xp.shape[0], xs.shape[0]
    n = n_p + n_s
    tm = ROW_TILE
    npt, tps = n_p // tm, seq_s // tm
    d = D_MODEL
    cond_spec = pl.BlockSpec((None, 1, ADA_CHUNKS * d), lambda i: (_cond_index(i, npt, tps), 0, 0))
    rope_spec = pl.BlockSpec((tm, V7X_LANES), lambda i: (_rope_index(i, npt, tps), 0))
    prompt_spec = lambda width: pl.BlockSpec((tm, width), lambda i: (jnp.minimum(i, npt - 1), 0))
    return pl.pallas_call(
        functools.partial(_proj0_kernel, npt),
        grid=(n // tm,),
        in_specs=[prompt_spec(d),
                  pl.BlockSpec((tm, d), lambda i: (jnp.maximum(i - npt, 0), 0)),
                  cond_spec, _const_spec((1, d)), _const_spec(w_in.shape), _const_spec(gmat.shape),
                  _const_spec((1, A_Q_W)), _const_spec((1, A_KV_W)), rope_spec, rope_spec],
        out_specs=[_row_spec(tm, A_Q_W), _row_spec(tm, A_KV_W), _row_spec(tm, A_KV_W),
                   _row_spec(tm, LRU_WIDTH), _row_spec(tm, LRU_WIDTH),
                   prompt_spec(A_KV_W), prompt_spec(A_KV_W)],
        out_shape=[jax.ShapeDtypeStruct((n, A_Q_W), BF16), jax.ShapeDtypeStruct((n, A_KV_W), BF16),
                   jax.ShapeDtypeStruct((n, A_KV_W), BF16), jax.ShapeDtypeStruct((n, LRU_WIDTH), F32),
                   jax.ShapeDtypeStruct((n, LRU_WIDTH), F32),
                   jax.ShapeDtypeStruct((n_p, A_KV_W), F32), jax.ShapeDtypeStruct((n_p, A_KV_W), F32)],
        name="proj0",
        compiler_params=_params("arbitrary"),
    )(xp, xs, mod, nw, w_in, gmat, qg, kg, cos, sin)


def _proj1(x1, moe, n_p, seq_s, mod_prev, mod, nw, w_in, gmat, qg, kg, cos, sin):
    n = x1.shape[0]
    tm = ROW_TILE
    npt, tps = n_p // tm, seq_s // tm
    d = D_MODEL
    cond_spec = pl.BlockSpec((None, 1, ADA_CHUNKS * d), lambda i: (_cond_index(i, npt, tps), 0, 0))
    rope_spec = pl.BlockSpec((tm, V7X_LANES), lambda i: (_rope_index(i, npt, tps), 0))
    prompt_spec = lambda width: pl.BlockSpec((tm, width), lambda i: (jnp.minimum(i, npt - 1), 0))
    w = C_QK_W
    return pl.pallas_call(
        functools.partial(_proj1_kernel, npt),
        grid=(n // tm,),
        in_specs=[_row_spec(tm, d), _row_spec(tm, d), cond_spec, cond_spec, _const_spec((1, d)),
                  _const_spec(w_in.shape), _const_spec(gmat.shape),
                  _const_spec((1, w)), _const_spec((1, w)), rope_spec, rope_spec],
        out_specs=[_row_spec(tm, d), _row_spec(tm, w), _row_spec(tm, w), _row_spec(tm, w),
                   prompt_spec(w), prompt_spec(w)],
        out_shape=[jax.ShapeDtypeStruct((n, d), F32), jax.ShapeDtypeStruct((n, w), BF16),
                   jax.ShapeDtypeStruct((n, w), BF16), jax.ShapeDtypeStruct((n, w), BF16),
                   jax.ShapeDtypeStruct((n_p, w), F32), jax.ShapeDtypeStruct((n_p, w), F32)],
        name="proj1",
        compiler_params=_params("arbitrary"),
    )(x1, moe, mod_prev, mod, nw, w_in, gmat, qg, kg, cos, sin)


def _softmax_parts(q, keys):
    ss = [_dot_nt(q, k) for k in keys]
    m = ss[0].max(axis=-1, keepdims=True)
    for s in ss[1:]:
        m = jnp.maximum(m, s.max(axis=-1, keepdims=True))
    es = [jnp.exp(s - m) for s in ss]
    l = es[0].sum(axis=-1, keepdims=True)
    for e in es[1:]:
        l = l + e.sum(axis=-1, keepdims=True)
    return es, 1.0 / l


def _gqa_kernel(has_cache, *refs):
    if has_cache:
        q_ref, kn_ref, vn_ref, kc_ref, vc_ref, _, o_ref = refs
    else:
        q_ref, kn_ref, vn_ref, o_ref = refs
    grp = A_HEADS // A_KV_HEADS
    hd = HEAD_DIM
    for g in range(A_KV_HEADS):
        lanes = slice(g * hd, (g + 1) * hd)
        keys = [kn_ref[:, lanes]]
        vals = [vn_ref[:, lanes]]
        if has_cache:
            keys = [kc_ref[:, lanes].astype(BF16)] + keys
            vals = [vc_ref[:, lanes].astype(BF16)] + vals
        for j in range(grp):
            h = g * grp + j
            es, rl = _softmax_parts(q_ref[:, h * hd:(h + 1) * hd], keys)
            o = _dot(es[0].astype(BF16), vals[0])
            for e, v in zip(es[1:], vals[1:]):
                o = o + _dot(e.astype(BF16), v)
            o_ref[:, h * hd:(h + 1) * hd] = (o * rl).astype(BF16)


def _diff_kernel(has_cache, *refs):
    if has_cache:
        lam_ref, sub_ref, q_ref, kn_ref, vn_ref, kc_ref, vc_ref, _, o_ref = refs
    else:
        lam_ref, sub_ref, q_ref, kn_ref, vn_ref, o_ref = refs
    lam_p = lam_ref[...]
    lam = (jnp.exp(jnp.sum(lam_p[0:1] * lam_p[1:2], axis=-1, keepdims=True))
           - jnp.exp(jnp.sum(lam_p[2:3] * lam_p[3:4], axis=-1, keepdims=True)) + LAM_INIT)
    hd = HEAD_DIM
    vd = 2 * HEAD_DIM
    sub = sub_ref[...] * (1.0 - LAM_INIT)
    for h in range(C_HEADS):
        ws = None
        for p in range(2):
            lanes = slice((p * C_HEADS + h) * hd, (p * C_HEADS + h + 1) * hd)
            keys = [kn_ref[:, lanes]]
            if has_cache:
                keys = [kc_ref[:, lanes].astype(BF16)] + keys
            es, rl = _softmax_parts(q_ref[:, lanes], keys)
            if p == 0:
                ws = [e * rl for e in es]
            else:
                rl = rl * lam
                ws = [w - e * rl for w, e in zip(ws, es)]
        vlanes = slice(h * vd, (h + 1) * vd)
        vals = [vn_ref[:, vlanes]]
        if has_cache:
            vals = [vc_ref[:, vlanes].astype(BF16)] + vals
        o = _dot(ws[0].astype(BF16), vals[0])
        for w, v in zip(ws[1:], vals[1:]):
            o = o + _dot(w.astype(BF16), v)
        ms = jnp.mean(o * o, axis=-1, keepdims=True)
        o_ref[:, vlanes] = (o * lax.rsqrt(ms + EPS) * sub).astype(BF16)


def _attention(name, kernel, extra, q, k, v, cache_k, cache_v, n_p, seq_p, seq_s, out_w):
    n = q.shape[0]
    qw, kw, vw = q.shape[1], k.shape[1], v.shape[1]
    n_extra = len(extra)
    extra_specs = [pl.BlockSpec(e.shape, lambda *idx, nd=e.ndim: (0,) * nd) for e in extra]
    out_shape = jax.ShapeDtypeStruct((n, out_w), BF16)
    o_p = pl.pallas_call(
        functools.partial(kernel, False),
        grid=(n_p // seq_p,),
        in_specs=extra_specs + [_row_spec(seq_p, qw), _row_spec(seq_p, kw), _row_spec(seq_p, vw)],
        out_specs=_row_spec(seq_p, out_w),
        out_shape=out_shape,
        name=name + "_prompt",
        compiler_params=_params("parallel"),
    )(*extra, q, k, v)
    past = cache_k.shape[0] // ((n - n_p) // seq_s)
    tq = Q_TILE
    qb0, qbs, sb0 = n_p // tq, seq_s // tq, n_p // seq_s
    return pl.pallas_call(
        functools.partial(kernel, True),
        grid=((n - n_p) // seq_s, qbs),
        in_specs=extra_specs + [
            pl.BlockSpec((tq, qw), lambda b, j: (qb0 + b * qbs + j, 0)),
            pl.BlockSpec((seq_s, kw), lambda b, j: (sb0 + b, 0)),
            pl.BlockSpec((seq_s, vw), lambda b, j: (sb0 + b, 0)),
            pl.BlockSpec((past, kw), lambda b, j: (b, 0)),
            pl.BlockSpec((past, vw), lambda b, j: (b, 0)),
            pl.BlockSpec(memory_space=pl.ANY)],
        out_specs=pl.BlockSpec((tq, out_w), lambda b, j: (qb0 + b * qbs + j, 0)),
        out_shape=out_shape,
        input_output_aliases={n_extra + 5: 0},
        name=name + "_sample",
        compiler_params=_params("parallel", "arbitrary"),
    )(*extra, q, k, v, cache_k, cache_v, o_p)


def _gelu(x):
    return 0.5 * x * (1.0 + jnp.tanh(math.sqrt(2.0 / math.pi) * (x + 0.044715 * (x * x * x))))


def _lru_gates(xc, d, wg_ref, vec_ref):
    half = V7X_MXU_DIM
    xcb = xc.astype(BF16)
    g0 = _dot(xcb[:, 0:half], wg_ref[d, 0])
    g1 = _dot(xcb[:, half:], wg_ref[d, 1])
    ga = jnp.concatenate([g0[:, 0:half], g1[:, 0:half]], axis=1)
    gi = jnp.concatenate([g0[:, half:], g1[:, half:]], axis=1)
    vec = vec_ref[d]
    r = _sigmoid(ga + vec[0:1])
    gate_i = _sigmoid(gi + vec[1:2])
    z = -vec[2:3]
    softplus = jnp.maximum(z, 0.0) + jnp.log1p(jnp.exp(-jnp.abs(z)))
    log_a = (-RG_C * softplus) * r
    a = jnp.exp(log_a)
    b = jnp.sqrt(1.0 - a * a) * (gate_i * xc)
    return a, b


def _group_scan(a, b, reverse):
    row = lax.broadcasted_iota(jnp.int32, a.shape, 0) % V7X_SUBLANES
    for k in (1, 2, 4):
        if reverse:
            shift, live = a.shape[0] - k, row < V7X_SUBLANES - k
        else:
            shift, live = k, row >= k
        a_n = pltpu.roll(a, shift, 0)
        b_n = pltpu.roll(b, shift, 0)
        b = jnp.where(live, a * b_n + b, b)
        a = jnp.where(live, a * a_n, a)
    return a, b


def _lru_kernel(seq, xb_ref, gb_ref, cw_ref, wg_ref, vec_ref, h0_ref, rec_ref, st_ref,
                xpad_ref, xc_ref, hf_ref, a_ref, b_ref):
    pad = V7X_SUBLANES
    ch = SCAN_CHUNK
    n_chunks = seq // ch
    groups = ch // V7X_SUBLANES
    width = LRU_WIDTH
    zeros = jnp.zeros((pad, width), F32)
    xpad_ref[0:pad, :] = zeros
    xpad_ref[pad + seq:2 * pad + seq, :] = zeros
    xpad_ref[pad:pad + seq, :] = xb_ref[...]
    cw = cw_ref[...]
    xc_ref[...] = (cw[4:5] + cw[0:1] * xpad_ref[pad - 1:pad - 1 + seq, :] + cw[1:2] * xpad_ref[pad:pad + seq, :]
                   + cw[2:3] * xpad_ref[pad + 1:pad + 1 + seq, :] + cw[3:4] * xpad_ref[pad + 2:pad + 2 + seq, :])
    h0 = h0_ref[...]

    def chunk(c, h, reverse):
        r0 = pl.multiple_of(c * ch, ch)
        xc = xc_ref[pl.ds(r0, ch), :]
        a, b = _lru_gates(xc, 1 if reverse else 0, wg_ref, vec_ref)
        a, b = _group_scan(a, b, reverse)
        a_ref[...] = a
        b_ref[...] = b

        def group(t, h):
            g = (groups - 1 - t) if reverse else t
            g0 = pl.multiple_of(g * V7X_SUBLANES, V7X_SUBLANES)
            hs = a_ref[pl.ds(g0, V7X_SUBLANES), :] * h + b_ref[pl.ds(g0, V7X_SUBLANES), :]
            b_ref[pl.ds(g0, V7X_SUBLANES), :] = hs
            return hs[0:1] if reverse else hs[V7X_SUBLANES - 1:V7X_SUBLANES]

        h = lax.fori_loop(0, groups, group, h)
        if reverse:
            rec = (hf_ref[pl.ds(r0, ch), :] + b_ref[...]) * _gelu(gb_ref[pl.ds(r0, ch), :])
            rec_ref[pl.ds(r0, ch), :] = rec.astype(BF16)
        else:
            hf_ref[pl.ds(r0, ch), :] = b_ref[...]
        return h

    hf = lax.fori_loop(0, n_chunks, lambda c, h: chunk(c, h, False), h0[0:1])
    hb = lax.fori_loop(0, n_chunks, lambda c, h: chunk(n_chunks - 1 - c, h, True), h0[1:2])
    st_ref[0:1, :] = hf
    st_ref[1:2, :] = hb


def _lru_call(xb, gb, row0, n_seq, seq, cw, wg, vec, h0, rec_prev):
    n = xb.shape[0]
    width = LRU_WIDTH
    blk0 = row0 // seq
    seq_spec = pl.BlockSpec((seq, width), lambda s: (blk0 + s, 0))
    in_specs = [seq_spec, seq_spec, _const_spec(cw.shape), _const_spec(wg.shape), _const_spec(vec.shape),
                pl.BlockSpec((None, 2, width), lambda s: (s, 0, 0))]
    args = [xb, gb, cw, wg, vec, h0]
    aliases = {}
    if rec_prev is not None:
        in_specs.append(pl.BlockSpec(memory_space=pl.ANY))
        args.append(rec_prev)
        aliases = {6: 0}
    kernel = functools.partial(_lru_kernel, seq)
    if rec_prev is not None:
        kernel = lambda *r: _lru_kernel(seq, *r[:6], *r[7:])
    return pl.pallas_call(
        kernel,
        grid=(n_seq,),
        in_specs=in_specs,
        out_specs=[seq_spec, pl.BlockSpec((None, 2, width), lambda s: (s, 0, 0))],
        out_shape=[jax.ShapeDtypeStruct((n, width), BF16), jax.ShapeDtypeStruct((n_seq, 2, width), F32)],
        scratch_shapes=[pltpu.VMEM((seq + 2 * V7X_SUBLANES, width), F32), pltpu.VMEM((seq, width), F32),
                        pltpu.VMEM((seq, width), F32), pltpu.VMEM((SCAN_CHUNK, width), F32),
                        pltpu.VMEM((SCAN_CHUNK, width), F32)],
        input_output_aliases=aliases,
        name="lru_seq%d" % seq,
        compiler_params=_params("parallel"),
    )(*args)


def _lru_gate_weights(wa, wi):
    per_half = V7X_MXU_DIM // LRU_BLOCK_W
    halves = []
    for hh in range(LRU_WIDTH // V7X_MXU_DIM):
        cols = []
        for w in (wa, wi):
            m = jnp.zeros((V7X_MXU_DIM, V7X_MXU_DIM), F32)
            for j in range(per_half):
                blk = w[hh * per_half + j]
                m = lax.dynamic_update_slice(m, blk, (j * LRU_BLOCK_W, j * LRU_BLOCK_W))
            cols.append(m)
        halves.append(jnp.concatenate(cols, axis=1))
    return jnp.stack(halves).astype(BF16)


def _outproj_kernel(npt, n_parts, dual, *refs):
    parts = refs[:n_parts]
    refs = refs[n_parts:]
    if dual:
        xp_ref, xs_ref, w_ref, mod_ref, nw_ref, r_ref, x1_ref, h2_ref, aff_ref = refs
        i = pl.program_id(0)
        x = jnp.where(i < npt, xp_ref[...], xs_ref[...])
    else:
        x_ref, w_ref, mod_ref, nw_ref, r_ref, x1_ref, h2_ref, aff_ref = refs
        x = x_ref[...]
    d = D_MODEL
    acc = None
    r0 = 0
    for p in parts:
        kw = p.shape[1]
        t = _dot(p[...], w_ref[r0:r0 + kw, :])
        acc = t if acc is None else acc + t
        r0 += kw
    mod = mod_ref[...]
    x1 = x + mod[:, 2 * d:3 * d] * acc
    x1_ref[...] = x1
    h2 = _rms_mod(x1, nw_ref[...], mod[:, 3 * d:4 * d], mod[:, 4 * d:5 * d]).astype(BF16)
    h2_ref[...] = h2
    logits = _dot(h2, r_ref[...])
    lane = lax.broadcasted_iota(jnp.int32, logits.shape, 1)
    logits = jnp.where(lane < N_EXPERTS, logits, -jnp.inf)
    e = jnp.exp(logits - logits.max(axis=-1, keepdims=True))
    aff_ref[...] = e / e.sum(axis=-1, keepdims=True)


def _outproj(parts, x_args, n_p, seq_s, w_out, mod, nw, router):
    n = parts[0].shape[0]
    tm = ROW_TILE
    npt, tps = n_p // tm, seq_s // tm
    d = D_MODEL
    dual = len(x_args) == 2
    cond_spec = pl.BlockSpec((None, 1, ADA_CHUNKS * d), lambda i: (_cond_index(i, npt, tps), 0, 0))
    if dual:
        x_specs = [pl.BlockSpec((tm, d), lambda i: (jnp.minimum(i, npt - 1), 0)),
                   pl.BlockSpec((tm, d), lambda i: (jnp.maximum(i - npt, 0), 0))]
    else:
        x_specs = [_row_spec(tm, d)]
    return pl.pallas_call(
        functools.partial(_outproj_kernel, npt, len(parts), dual),
        grid=(n // tm,),
        in_specs=[_row_spec(tm, p.shape[1]) for p in parts] + x_specs + [
            _const_spec(w_out.shape), cond_spec, _const_spec((1, d)), _const_spec(router.shape)],
        out_specs=[_row_spec(tm, d), _row_spec(tm, d), _row_spec(tm, ROUTER_PAD)],
        out_shape=[jax.ShapeDtypeStruct((n, d), F32), jax.ShapeDtypeStruct((n, d), BF16),
                   jax.ShapeDtypeStruct((n, ROUTER_PAD), F32)],
        name="outproj",
        compiler_params=_params("parallel"),
    )(*parts, *x_args, w_out, mod, nw, router)


def _ffn_kernel(xe_ref, gate_ref, wg_ref, wu_ref, wd_ref, o_ref):
    x = xe_ref[...]
    a = _dot(x, wg_ref[...])
    hid = ((a * _sigmoid(a)) * _dot(x, wu_ref[...])).astype(BF16)
    o_ref[...] = _dot(hid, wd_ref[...]) * gate_ref[...]


def _ffn(xe, gate, wg, wu, wd):
    n_e, cap, d = xe.shape
    ff = wg.shape[2]
    tm = FFN_TILE
    return pl.pallas_call(
        _ffn_kernel,
        grid=(n_e, cap // tm),
        in_specs=[pl.BlockSpec((None, tm, d), lambda e, t: (e, t, 0)),
                  pl.BlockSpec((None, tm, 1), lambda e, t: (e, t, 0)),
                  pl.BlockSpec((None, d, ff), lambda e, t: (e, 0, 0)),
                  pl.BlockSpec((None, d, ff), lambda e, t: (e, 0, 0)),
                  pl.BlockSpec((None, ff, d), lambda e, t: (e, 0, 0))],
        out_specs=pl.BlockSpec((None, tm, d), lambda e, t: (e, t, 0)),
        out_shape=jax.ShapeDtypeStruct((n_e, cap, d), F32),
        name="expert_ffn",
        compiler_params=_params("parallel", "arbitrary"),
    )(xe, gate, wg, wu, wd)


def _moe(h2, aff, n_p, wg, wu, wd):
    n = h2.shape[0]
    idxs, gates = [], []
    for r0, r1 in ((0, n_p), (n_p, n)):
        cap = EC_FACTOR * (r1 - r0) // N_EXPERTS
        gate, idx = lax.top_k(aff[r0:r1, :N_EXPERTS].T, cap)
        idxs.append(idx + r0)
        gates.append(gate)
    idx = jnp.concatenate(idxs, axis=1)
    gate = jnp.concatenate(gates, axis=1)
    xe = jnp.take(h2, idx, axis=0)
    ye = _ffn(xe, gate[..., None], wg, wu, wd)
    return jnp.zeros((n, D_MODEL), F32).at[idx.reshape(-1)].add(ye.reshape(-1, D_MODEL))


def _final_kernel(npt, x1_ref, moe_ref, mod_ref, yp_ref, ys_ref):
    i = pl.program_id(0)
    d = D_MODEL
    y = x1_ref[...] + mod_ref[:, 5 * d:6 * d] * moe_ref[...]

    @pl.when(i < npt)
    def _():
        yp_ref[...] = y

    @pl.when(i >= npt)
    def _():
        ys_ref[...] = y


def _final(x1, moe, n_p, seq_s, mod):
    n = x1.shape[0]
    tm = ROW_TILE
    npt, tps = n_p // tm, seq_s // tm
    d = D_MODEL
    cond_spec = pl.BlockSpec((None, 1, ADA_CHUNKS * d), lambda i: (_cond_index(i, npt, tps), 0, 0))
    return pl.pallas_call(
        functools.partial(_final_kernel, npt),
        grid=(n // tm,),
        in_specs=[_row_spec(tm, d), _row_spec(tm, d), cond_spec],
        out_specs=[pl.BlockSpec((tm, d), lambda i: (jnp.minimum(i, npt - 1), 0)),
                   pl.BlockSpec((tm, d), lambda i: (jnp.maximum(i - npt, 0), 0))],
        out_shape=[jax.ShapeDtypeStruct((n_p, d), F32), jax.ShapeDtypeStruct((n - n_p, d), F32)],
        name="final_residual",
        compiler_params=_params("arbitrary"),
    )(x1, moe, mod)


def kernel(x_prompt, x_sample, cache_l0_k, cache_l0_v, state_l0_fwd, state_l0_bwd, cache_l1_k, cache_l1_v, c, c_ctx, l0_norm1, l0_norm2, l0_ada_w, l0_ada_b, l0_w_in, l0_q_gain, l0_k_gain, l0_conv_w, l0_conv_b, l0_fwd_wa, l0_fwd_ba, l0_fwd_wi, l0_fwd_bi, l0_fwd_lam, l0_bwd_wa, l0_bwd_ba, l0_bwd_wi, l0_bwd_bi, l0_bwd_lam, l0_w_out, l0_router, l0_w_gate, l0_w_up, l0_w_down, l1_norm1, l1_norm2, l1_ada_w, l1_ada_b, l1_w_in, l1_q_gain, l1_k_gain, l1_lam_q1, l1_lam_k1, l1_lam_q2, l1_lam_k2, l1_subln, l1_w_out, l1_router, l1_w_gate, l1_w_up, l1_w_down):
    d = D_MODEL
    bp, seq_p, _ = x_prompt.shape
    bs, seq_s, _ = x_sample.shape
    n_p, n_s = bp * seq_p, bs * seq_s
    xp = x_prompt.reshape(n_p, d)
    xs = x_sample.reshape(n_s, d)

    cond = jnp.zeros((COND_ROWS, d), F32).at[0].set(c_ctx).at[1:1 + bs].set(c)
    mod0 = _adaln(cond, l0_ada_w, l0_ada_b).reshape(COND_ROWS, 1, ADA_CHUNKS * d)
    mod1 = _adaln(cond, l1_ada_w, l1_ada_b).reshape(COND_ROWS, 1, ADA_CHUNKS * d)
    gmat = _group_matrix()
    cos, sin = _rope_tables(seq_s, ROW_TILE)
    row = lambda v: v.reshape(1, -1)
    tile_gain = lambda g, width: jnp.tile(g, width // HEAD_DIM).reshape(1, width)
    pad_router = lambda r: jnp.pad(r, ((0, 0), (0, ROUTER_PAD - N_EXPERTS))).astype(BF16)

    q, k, v, xb, gb, k0, v0 = _proj0(xp, xs, seq_s, mod0, row(l0_norm1), l0_w_in.astype(BF16), gmat,
                                     tile_gain(l0_q_gain, A_Q_W), tile_gain(l0_k_gain, A_KV_W), cos, sin)
    attn = _attention("gqa", _gqa_kernel, [], q, k, v, cache_l0_k.reshape(-1, A_KV_W), cache_l0_v.reshape(-1, A_KV_W),
                      n_p, seq_p, seq_s, A_Q_W)
    cw = jnp.concatenate([l0_conv_w, l0_conv_b[None], jnp.zeros((3, LRU_WIDTH), F32)], axis=0)
    wg = jnp.stack([_lru_gate_weights(l0_fwd_wa, l0_fwd_wi), _lru_gate_weights(l0_bwd_wa, l0_bwd_wi)])
    pad5 = jnp.zeros((5, LRU_WIDTH), F32)
    vec = jnp.stack([jnp.concatenate([l0_fwd_ba[None], l0_fwd_bi[None], l0_fwd_lam[None], pad5], axis=0),
                     jnp.concatenate([l0_bwd_ba[None], l0_bwd_bi[None], l0_bwd_lam[None], pad5], axis=0)])
    rec, st = _lru_call(xb, gb, 0, bp, seq_p, cw, wg, vec, jnp.zeros((bp, 2, LRU_WIDTH), F32), None)
    rec, _ = _lru_call(xb, gb, n_p, bs, seq_s, cw, wg, vec,
                       jnp.stack([state_l0_fwd, state_l0_bwd], axis=1), rec)
    x1, h2, aff = _outproj([attn, rec], [xp, xs], n_p, seq_s, l0_w_out.astype(BF16), mod0, row(l0_norm2),
                           pad_router(l0_router))
    moe = _moe(h2, aff, n_p, l0_w_gate.astype(BF16), l0_w_up.astype(BF16), l0_w_down.astype(BF16))

    x2, q, k, v, k1, v1 = _proj1(x1, moe, n_p, seq_s, mod0, mod1, row(l1_norm1), l1_w_in.astype(BF16), gmat,
                                 tile_gain(l1_q_gain, C_QK_W), tile_gain(l1_k_gain, C_QK_W), cos, sin)
    lam_p = jnp.concatenate([l1_lam_q1[None], l1_lam_k1[None], l1_lam_q2[None], l1_lam_k2[None],
                             jnp.zeros((4, HEAD_DIM), F32)], axis=0)
    o = _attention("diff", _diff_kernel, [lam_p, row(l1_subln)], q, k, v, cache_l1_k.reshape(-1, C_QK_W),
                   cache_l1_v.reshape(-1, C_QK_W), n_p, seq_p, seq_s, C_QK_W)
    x1, h2, aff = _outproj([o], [x2], n_p, seq_s, l1_w_out.astype(BF16), mod1, row(l1_norm2),
                           pad_router(l1_router))
    moe = _moe(h2, aff, n_p, l1_w_gate.astype(BF16), l1_w_up.astype(BF16), l1_w_down.astype(BF16))
    y_p, y_s = _final(x1, moe, n_p, seq_s, mod1)

    return (y_p.reshape(bp, seq_p, d), y_s.reshape(bs, seq_s, d),
            k0.reshape(bp, seq_p, A_KV_HEADS, HEAD_DIM), v0.reshape(bp, seq_p, A_KV_HEADS, HEAD_DIM),
            st[:, 0], st[:, 1],
            k1.reshape(bp, seq_p, 2, C_HEADS, HEAD_DIM), v1.reshape(bp, seq_p, C_HEADS, 2 * HEAD_DIM))
```

```python
import functools
import math

import jax
import jax.numpy as jnp
from jax import lax
from jax.experimental import pallas as pl
from jax.experimental.pallas import tpu as pltpu

F32 = jnp.float32
BF16 = jnp.bfloat16

D_MODEL = 1024
HEAD_DIM = 64
GRID_W = 64
ROPE_FREQS = HEAD_DIM // 4
ROPE_THETA = 10000.0
EPS = 1e-6
A_HEADS = 8
A_KV_HEADS = 2
A_Q_W = A_HEADS * HEAD_DIM
A_KV_W = A_KV_HEADS * HEAD_DIM
LRU_WIDTH = 512
LRU_BLOCKS = 8
LRU_BLOCK_W = LRU_WIDTH // LRU_BLOCKS
RG_C = 8.0
C_HEADS = 8
C_QK_W = 2 * C_HEADS * HEAD_DIM
N_EXPERTS = 16
EC_FACTOR = 2
ADA_CHUNKS = 6
LAM_INIT = 0.8 - 0.6 * math.exp(-0.3)

V7X_LANES = 128
V7X_SUBLANES = 8
V7X_MXU_DIM = 256
V7X_VMEM_LIMIT_BYTES = 56 * 1024 * 1024

COND_ROWS = 16
ROW_TILE = 512
Q_TILE = 256
FFN_TILE = 512
SCAN_CHUNK = 256
ROUTER_PAD = V7X_LANES
GATHER_CHUNK = 1024
COMBINE_TILE = 128


def _params(*sem):
    return pltpu.CompilerParams(dimension_semantics=sem, vmem_limit_bytes=V7X_VMEM_LIMIT_BYTES)


def _dot(a, b):
    return jnp.dot(a, b, preferred_element_type=F32)


def _dot_nt(a, b):
    return lax.dot_general(a, b, (((1,), (1,)), ((), ())), preferred_element_type=F32)


def _sigmoid(x):
    return 1.0 / (1.0 + jnp.exp(-x))


def _adaln_kernel(c_ref, w_ref, b_ref, o_ref):
    c = c_ref[...]
    s = (c * _sigmoid(c)).astype(BF16)
    o_ref[...] = _dot(s, w_ref[...].astype(BF16)) + b_ref[...]


def _adaln(cond, w, b):
    d, n = w.shape
    tn = 1024
    return pl.pallas_call(
        _adaln_kernel,
        grid=(n // tn,),
        in_specs=[pl.BlockSpec((COND_ROWS, d), lambda j: (0, 0)),
                  pl.BlockSpec((d, tn), lambda j: (0, j)),
                  pl.BlockSpec((1, tn), lambda j: (0, j))],
        out_specs=pl.BlockSpec((COND_ROWS, tn), lambda j: (0, j)),
        out_shape=jax.ShapeDtypeStruct((COND_ROWS, n), F32),
        name="adaln",
        compiler_params=_params("parallel"),
    )(cond, w, b.reshape(1, n))


def _rms_mod(x, nw, shift, scale):
    ms = jnp.mean(x * x, axis=-1, keepdims=True)
    return (x * lax.rsqrt(ms + EPS) * nw) * (1.0 + scale) + shift


def _head_norm(y, g_ref, gain):
    width = y.shape[1]
    outs = []
    for c0 in range(0, width, V7X_MXU_DIM):
        w = min(V7X_MXU_DIM, width - c0)
        ys = y[:, c0:c0 + w]
        sq = ys * ys
        hi = sq.astype(BF16)
        lo = (sq - hi.astype(F32)).astype(BF16)
        g = g_ref[0:w, 0:w]
        ms = (_dot(hi, g) + _dot(lo, g)) * (1.0 / HEAD_DIM)
        outs.append(ys * lax.rsqrt(ms + EPS))
    out = outs[0] if len(outs) == 1 else jnp.concatenate(outs, axis=1)
    return out * gain


def _rope(x, cos, sin):
    outs = []
    lane = lax.broadcasted_iota(jnp.int32, (x.shape[0], V7X_LANES), 1)
    first = (lane % (2 * ROPE_FREQS)) < ROPE_FREQS
    for c0 in range(0, x.shape[1], V7X_LANES):
        xs = x[:, c0:c0 + V7X_LANES]
        up = pltpu.roll(xs, V7X_LANES - ROPE_FREQS, 1)
        dn = pltpu.roll(xs, ROPE_FREQS, 1)
        outs.append(xs * cos + jnp.where(first, up, dn) * sin)
    return outs[0] if len(outs) == 1 else jnp.concatenate(outs, axis=1)


def _cond_index(i, n_prompt_tiles, tiles_per_sample):
    return jnp.where(i < n_prompt_tiles, 0, 1 + (i - n_prompt_tiles) // tiles_per_sample)


def _rope_index(i, n_prompt_tiles, tiles_per_sample):
    return jnp.where(i < n_prompt_tiles, tiles_per_sample, (i - n_prompt_tiles) % tiles_per_sample)


def _proj0_kernel(npt, xp_ref, xs_ref, mod_ref, nw_ref, w_ref, g_ref, qg_ref, kg_ref, cos_ref, sin_ref,
                  q_ref, k_ref, v_ref, xb_ref, gb_ref, kp_ref, vp_ref):
    i = pl.program_id(0)
    x = jnp.where(i < npt, xp_ref[...], xs_ref[...])
    mod = mod_ref[...]
    d = D_MODEL
    h = _rms_mod(x, nw_ref[...], mod[:, 0:d], mod[:, d:2 * d]).astype(BF16)
    cos = cos_ref[...]
    sin = sin_ref[...]
    o1, o2, o3, o4 = A_Q_W, A_Q_W + A_KV_W, A_Q_W + 2 * A_KV_W, A_Q_W + 2 * A_KV_W + LRU_WIDTH
    q = _head_norm(_dot(h, w_ref[:, 0:o1]), g_ref, qg_ref[...])
    q_ref[...] = (_rope(q, cos, sin) * (HEAD_DIM ** -0.5)).astype(BF16)
    kn = _head_norm(_dot(h, w_ref[:, o1:o2]), g_ref, kg_ref[...])
    k = _rope(kn, cos, sin)
    k_ref[...] = k.astype(BF16)
    v = _dot(h, w_ref[:, o2:o3])
    v_ref[...] = v.astype(BF16)
    xb_ref[...] = _dot(h, w_ref[:, o3:o4])
    gb_ref[...] = _dot(h, w_ref[:, o4:])

    @pl.when(i < npt)
    def _():
        kp_ref[...] = k
        vp_ref[...] = v


def _proj1_kernel(npt, x_ref, mod_ref, nw_ref, w_ref, g_ref, qg_ref, kg_ref, cos_ref, sin_ref,
                  q_ref, k_ref, v_ref, kp_ref, vp_ref):
    i = pl.program_id(0)
    d = D_MODEL
    x = x_ref[...]
    mod = mod_ref[...]
    h = _rms_mod(x, nw_ref[...], mod[:, 0:d], mod[:, d:2 * d]).astype(BF16)
    cos = cos_ref[...]
    sin = sin_ref[...]
    w = C_QK_W
    q = _head_norm(_dot(h, w_ref[:, 0:w]), g_ref, qg_ref[...])
    q_ref[...] = (_rope(q, cos, sin) * (HEAD_DIM ** -0.5)).astype(BF16)
    kn = _head_norm(_dot(h, w_ref[:, w:2 * w]), g_ref, kg_ref[...])
    k = _rope(kn, cos, sin)
    k_ref[...] = k.astype(BF16)
    v = _dot(h, w_ref[:, 2 * w:3 * w])
    v_ref[...] = v.astype(BF16)

    @pl.when(i < npt)
    def _():
        kp_ref[...] = k
        vp_ref[...] = v


def _group_matrix():
    idx = jnp.arange(V7X_MXU_DIM) // HEAD_DIM
    return (idx[:, None] == idx[None, :]).astype(BF16)


def _rope_tables(seq, tile):
    rows = seq // GRID_W
    pos_r = jnp.repeat(jnp.arange(rows, dtype=F32), GRID_W)
    pos_c = jnp.tile(jnp.arange(GRID_W, dtype=F32), rows)
    inv = ROPE_THETA ** (-jnp.arange(ROPE_FREQS, dtype=F32) / ROPE_FREQS)
    ang_r = pos_r[:, None] * inv
    ang_c = pos_c[:, None] * inv
    cos = jnp.concatenate([jnp.cos(ang_r)] * 2 + [jnp.cos(ang_c)] * 2, axis=1)
    sin = jnp.concatenate([-jnp.sin(ang_r), jnp.sin(ang_r), -jnp.sin(ang_c), jnp.sin(ang_c)], axis=1)
    reps = V7X_LANES // HEAD_DIM
    cos = jnp.concatenate([jnp.tile(cos, (1, reps)), jnp.ones((tile, V7X_LANES), F32)], axis=0)
    sin = jnp.concatenate([jnp.tile(sin, (1, reps)), jnp.zeros((tile, V7X_LANES), F32)], axis=0)
    return cos, sin


def _row_spec(tile, width):
    return pl.BlockSpec((tile, width), lambda i: (i, 0))


def _const_spec(shape):
    return pl.BlockSpec(shape, lambda i: (0,) * len(shape))


def _proj0(xp, xs, seq_s, mod, nw, w_in, gmat, qg, kg, cos, sin):
    n_p, n_s = xp.shape[0], xs.shape[0]
    n = n_p + n_s
    tm = ROW_TILE
    npt, tps = n_p // tm, seq_s // tm
    d = D_MODEL
    cond_spec = pl.BlockSpec((None, 1, ADA_CHUNKS * d), lambda i: (_cond_index(i, npt, tps), 0, 0))
    rope_spec = pl.BlockSpec((tm, V7X_LANES), lambda i: (_rope_index(i, npt, tps), 0))
    prompt_spec = lambda width: pl.BlockSpec((tm, width), lambda i: (jnp.minimum(i, npt - 1), 0))
    return pl.pallas_call(
        functools.partial(_proj0_kernel, npt),
        grid=(n // tm,),
        in_specs=[prompt_spec(d),
                  pl.BlockSpec((tm, d), lambda i: (jnp.maximum(i - npt, 0), 0)),
                  cond_spec, _const_spec((1, d)), _const_spec(w_in.shape), _const_spec(gmat.shape),
                  _const_spec((1, A_Q_W)), _const_spec((1, A_KV_W)), rope_spec, rope_spec],
        out_specs=[_row_spec(tm, A_Q_W), _row_spec(tm, A_KV_W), _row_spec(tm, A_KV_W),
                   _row_spec(tm, LRU_WIDTH), _row_spec(tm, LRU_WIDTH),
                   prompt_spec(A_KV_W), prompt_spec(A_KV_W)],
        out_shape=[jax.ShapeDtypeStruct((n, A_Q_W), BF16), jax.ShapeDtypeStruct((n, A_KV_W), BF16),
                   jax.ShapeDtypeStruct((n, A_KV_W), BF16), jax.ShapeDtypeStruct((n, LRU_WIDTH), F32),
                   jax.ShapeDtypeStruct((n, LRU_WIDTH), F32),
                   jax.ShapeDtypeStruct((n_p, A_KV_W), F32), jax.ShapeDtypeStruct((n_p, A_KV_W), F32)],
        name="proj0",
        compiler_params=_params("arbitrary"),
    )(xp, xs, mod, nw, w_in, gmat, qg, kg, cos, sin)


def _proj1(x, n_p, seq_s, mod, nw, w_in, gmat, qg, kg, cos, sin):
    n = x.shape[0]
    tm = ROW_TILE
    npt, tps = n_p // tm, seq_s // tm
    d = D_MODEL
    cond_spec = pl.BlockSpec((None, 1, ADA_CHUNKS * d), lambda i: (_cond_index(i, npt, tps), 0, 0))
    rope_spec = pl.BlockSpec((tm, V7X_LANES), lambda i: (_rope_index(i, npt, tps), 0))
    prompt_spec = lambda width: pl.BlockSpec((tm, width), lambda i: (jnp.minimum(i, npt - 1), 0))
    w = C_QK_W
    return pl.pallas_call(
        functools.partial(_proj1_kernel, npt),
        grid=(n // tm,),
        in_specs=[_row_spec(tm, d), cond_spec, _const_spec((1, d)),
                  _const_spec(w_in.shape), _const_spec(gmat.shape),
                  _const_spec((1, w)), _const_spec((1, w)), rope_spec, rope_spec],
        out_specs=[_row_spec(tm, w), _row_spec(tm, w), _row_spec(tm, w),
                   prompt_spec(w), prompt_spec(w)],
        out_shape=[jax.ShapeDtypeStruct((n, w), BF16),
                   jax.ShapeDtypeStruct((n, w), BF16), jax.ShapeDtypeStruct((n, w), BF16),
                   jax.ShapeDtypeStruct((n_p, w), F32), jax.ShapeDtypeStruct((n_p, w), F32)],
        name="proj1",
        compiler_params=_params("arbitrary"),
    )(x, mod, nw, w_in, gmat, qg, kg, cos, sin)


def _softmax_parts(q, keys):
    ss = [_dot_nt(q, k) for k in keys]
    m = ss[0].max(axis=-1, keepdims=True)
    for s in ss[1:]:
        m = jnp.maximum(m, s.max(axis=-1, keepdims=True))
    es = [jnp.exp(s - m) for s in ss]
    l = es[0].sum(axis=-1, keepdims=True)
    for e in es[1:]:
        l = l + e.sum(axis=-1, keepdims=True)
    return es, 1.0 / l


def _gqa_kernel(has_cache, *refs):
    if has_cache:
        q_ref, kn_ref, vn_ref, kc_ref, vc_ref, o_ref = refs
    else:
        q_ref, kn_ref, vn_ref, o_ref = refs
    grp = A_HEADS // A_KV_HEADS
    hd = HEAD_DIM
    for g in range(A_KV_HEADS):
        lanes = slice(g * hd, (g + 1) * hd)
        keys = [kn_ref[:, lanes]]
        vals = [vn_ref[:, lanes]]
        if has_cache:
            keys = [kc_ref[:, lanes].astype(BF16)] + keys
            vals = [vc_ref[:, lanes].astype(BF16)] + vals
        for j in range(grp):
            h = g * grp + j
            es, rl = _softmax_parts(q_ref[:, h * hd:(h + 1) * hd], keys)
            o = _dot(es[0].astype(BF16), vals[0])
            for e, v in zip(es[1:], vals[1:]):
                o = o + _dot(e.astype(BF16), v)
            o_ref[:, h * hd:(h + 1) * hd] = (o * rl).astype(BF16)


def _diff_kernel(has_cache, *refs):
    if has_cache:
        lam_ref, sub_ref, q_ref, kn_ref, vn_ref, kc_ref, vc_ref, o_ref = refs
    else:
        lam_ref, sub_ref, q_ref, kn_ref, vn_ref, o_ref = refs
    lam_p = lam_ref[...]
    lam = (jnp.exp(jnp.sum(lam_p[0:1] * lam_p[1:2], axis=-1, keepdims=True))
           - jnp.exp(jnp.sum(lam_p[2:3] * lam_p[3:4], axis=-1, keepdims=True)) + LAM_INIT)
    hd = HEAD_DIM
    vd = 2 * HEAD_DIM
    sub = sub_ref[...] * (1.0 - LAM_INIT)
    for h in range(C_HEADS):
        ws = None
        for p in range(2):
            lanes = slice((p * C_HEADS + h) * hd, (p * C_HEADS + h + 1) * hd)
            keys = [kn_ref[:, lanes]]
            if has_cache:
                keys = [kc_ref[:, lanes].astype(BF16)] + keys
            es, rl = _softmax_parts(q_ref[:, lanes], keys)
            if p == 0:
                ws = [e * rl for e in es]
            else:
                rl = rl * lam
                ws = [w - e * rl for w, e in zip(ws, es)]
        vlanes = slice(h * vd, (h + 1) * vd)
        vals = [vn_ref[:, vlanes]]
        if has_cache:
            vals = [vc_ref[:, vlanes].astype(BF16)] + vals
        o = _dot(ws[0].astype(BF16), vals[0])
        for w, v in zip(ws[1:], vals[1:]):
            o = o + _dot(w.astype(BF16), v)
        ms = jnp.mean(o * o, axis=-1, keepdims=True)
        o_ref[:, vlanes] = (o * lax.rsqrt(ms + EPS) * sub).astype(BF16)


def _attention(name, kernel, extra, q, k, v, cache_k, cache_v, n_p, seq_p, seq_s, out_w):
    n = q.shape[0]
    qw, kw, vw = q.shape[1], k.shape[1], v.shape[1]
    extra_specs = [pl.BlockSpec(e.shape, lambda *idx, nd=e.ndim: (0,) * nd) for e in extra]
    o_p = pl.pallas_call(
        functools.partial(kernel, False),
        grid=(n_p // seq_p,),
        in_specs=extra_specs + [_row_spec(seq_p, qw), _row_spec(seq_p, kw), _row_spec(seq_p, vw)],
        out_specs=_row_spec(seq_p, out_w),
        out_shape=jax.ShapeDtypeStruct((n_p, out_w), BF16),
        name=name + "_prompt",
        compiler_params=_params("parallel"),
    )(*extra, q, k, v)
    past = cache_k.shape[0] // ((n - n_p) // seq_s)
    tq = Q_TILE
    qb0, qbs, sb0 = n_p // tq, seq_s // tq, n_p // seq_s
    o_s = pl.pallas_call(
        functools.partial(kernel, True),
        grid=((n - n_p) // seq_s, qbs),
        in_specs=extra_specs + [
            pl.BlockSpec((tq, qw), lambda b, j: (qb0 + b * qbs + j, 0)),
            pl.BlockSpec((seq_s, kw), lambda b, j: (sb0 + b, 0)),
            pl.BlockSpec((seq_s, vw), lambda b, j: (sb0 + b, 0)),
            pl.BlockSpec((past, kw), lambda b, j: (b, 0)),
            pl.BlockSpec((past, vw), lambda b, j: (b, 0))],
        out_specs=pl.BlockSpec((tq, out_w), lambda b, j: (b * qbs + j, 0)),
        out_shape=jax.ShapeDtypeStruct((n - n_p, out_w), BF16),
        name=name + "_sample",
        compiler_params=_params("parallel", "arbitrary"),
    )(*extra, q, k, v, cache_k, cache_v)
    return o_p, o_s


def _gelu(x):
    return 0.5 * x * (1.0 + jnp.tanh(math.sqrt(2.0 / math.pi) * (x + 0.044715 * (x * x * x))))


def _lru_gates(xc, d, wg_ref, vec_ref):
    half = V7X_MXU_DIM
    xcb = xc.astype(BF16)
    g0 = _dot(xcb[:, 0:half], wg_ref[d, 0])
    g1 = _dot(xcb[:, half:], wg_ref[d, 1])
    ga = jnp.concatenate([g0[:, 0:half], g1[:, 0:half]], axis=1)
    gi = jnp.concatenate([g0[:, half:], g1[:, half:]], axis=1)
    vec = vec_ref[d]
    r = _sigmoid(ga + vec[0:1])
    gate_i = _sigmoid(gi + vec[1:2])
    z = -vec[2:3]
    softplus = jnp.maximum(z, 0.0) + jnp.log1p(jnp.exp(-jnp.abs(z)))
    log_a = (-RG_C * softplus) * r
    a = jnp.exp(log_a)
    b = jnp.sqrt(1.0 - a * a) * (gate_i * xc)
    return a, b


def _group_scan(a, b, reverse):
    row = lax.broadcasted_iota(jnp.int32, a.shape, 0) % V7X_SUBLANES
    for k in (1, 2, 4):
        if reverse:
            shift, live = a.shape[0] - k, row < V7X_SUBLANES - k
        else:
            shift, live = k, row >= k
        a_n = pltpu.roll(a, shift, 0)
        b_n = pltpu.roll(b, shift, 0)
        b = jnp.where(live, a * b_n + b, b)
        a = jnp.where(live, a * a_n, a)
    return a, b


def _lru_kernel(seq, xb_ref, gb_ref, cw_ref, wg_ref, vec_ref, h0_ref, rec_ref, st_ref,
                xpad_ref, xc_ref, hf_ref, a_ref, b_ref):
    pad = V7X_SUBLANES
    ch = SCAN_CHUNK
    n_chunks = seq // ch
    groups = ch // V7X_SUBLANES
    width = LRU_WIDTH
    zeros = jnp.zeros((pad, width), F32)
    xpad_ref[0:pad, :] = zeros
    xpad_ref[pad + seq:2 * pad + seq, :] = zeros
    xpad_ref[pad:pad + seq, :] = xb_ref[...]
    cw = cw_ref[...]
    xc_ref[...] = (cw[4:5] + cw[0:1] * xpad_ref[pad - 1:pad - 1 + seq, :] + cw[1:2] * xpad_ref[pad:pad + seq, :]
                   + cw[2:3] * xpad_ref[pad + 1:pad + 1 + seq, :] + cw[3:4] * xpad_ref[pad + 2:pad + 2 + seq, :])
    h0 = h0_ref[...]

    def chunk(c, h, reverse):
        r0 = pl.multiple_of(c * ch, ch)
        xc = xc_ref[pl.ds(r0, ch), :]
        a, b = _lru_gates(xc, 1 if reverse else 0, wg_ref, vec_ref)
        a, b = _group_scan(a, b, reverse)
        a_ref[...] = a
        b_ref[...] = b

        def group(t, h):
            g = (groups - 1 - t) if reverse else t
            g0 = pl.multiple_of(g * V7X_SUBLANES, V7X_SUBLANES)
            hs = a_ref[pl.ds(g0, V7X_SUBLANES), :] * h + b_ref[pl.ds(g0, V7X_SUBLANES), :]
            b_ref[pl.ds(g0, V7X_SUBLANES), :] = hs
            return hs[0:1] if reverse else hs[V7X_SUBLANES - 1:V7X_SUBLANES]

        h = lax.fori_loop(0, groups, group, h)
        if reverse:
            rec = (hf_ref[pl.ds(r0, ch), :] + b_ref[...]) * _gelu(gb_ref[pl.ds(r0, ch), :])
            rec_ref[pl.ds(r0, ch), :] = rec.astype(BF16)
        else:
            hf_ref[pl.ds(r0, ch), :] = b_ref[...]
        return h

    hf = lax.fori_loop(0, n_chunks, lambda c, h: chunk(c, h, False), h0[0:1])
    hb = lax.fori_loop(0, n_chunks, lambda c, h: chunk(n_chunks - 1 - c, h, True), h0[1:2])
    st_ref[0:1, :] = hf
    st_ref[1:2, :] = hb


def _lru_call(xb, gb, row0, n_seq, seq, cw, wg, vec, h0):
    width = LRU_WIDTH
    blk0 = row0 // seq
    seq_spec = pl.BlockSpec((seq, width), lambda s: (blk0 + s, 0))
    return pl.pallas_call(
        functools.partial(_lru_kernel, seq),
        grid=(n_seq,),
        in_specs=[seq_spec, seq_spec, _const_spec(cw.shape), _const_spec(wg.shape), _const_spec(vec.shape),
                  pl.BlockSpec((None, 2, width), lambda s: (s, 0, 0))],
        out_specs=[pl.BlockSpec((seq, width), lambda s: (s, 0)), pl.BlockSpec((None, 2, width), lambda s: (s, 0, 0))],
        out_shape=[jax.ShapeDtypeStruct((n_seq * seq, width), BF16), jax.ShapeDtypeStruct((n_seq, 2, width), F32)],
        scratch_shapes=[pltpu.VMEM((seq + 2 * V7X_SUBLANES, width), F32), pltpu.VMEM((seq, width), F32),
                        pltpu.VMEM((seq, width), F32), pltpu.VMEM((SCAN_CHUNK, width), F32),
                        pltpu.VMEM((SCAN_CHUNK, width), F32)],
        name="lru_seq%d" % seq,
        compiler_params=_params("parallel"),
    )(xb, gb, cw, wg, vec, h0)


def _lru_gate_weights(wa, wi):
    per_half = V7X_MXU_DIM // LRU_BLOCK_W
    halves = []
    for hh in range(LRU_WIDTH // V7X_MXU_DIM):
        cols = []
        for w in (wa, wi):
            m = jnp.zeros((V7X_MXU_DIM, V7X_MXU_DIM), F32)
            for j in range(per_half):
                blk = w[hh * per_half + j]
                m = lax.dynamic_update_slice(m, blk, (j * LRU_BLOCK_W, j * LRU_BLOCK_W))
            cols.append(m)
        halves.append(jnp.concatenate(cols, axis=1))
    return jnp.stack(halves).astype(BF16)


def _outproj_kernel(npt, n_parts, dual, *refs):
    parts = refs[:2 * n_parts]
    refs = refs[2 * n_parts:]
    i = pl.program_id(0)
    if dual:
        xp_ref, xs_ref, w_ref, mod_ref, nw_ref, r_ref, x1_ref, h2_ref, aff_ref = refs
        x = jnp.where(i < npt, xp_ref[...], xs_ref[...])
    else:
        x_ref, w_ref, mod_ref, nw_ref, r_ref, x1_ref, h2_ref, aff_ref = refs
        x = x_ref[...]
    d = D_MODEL
    acc = None
    r0 = 0
    for pp, ps in zip(parts[0::2], parts[1::2]):
        kw = pp.shape[1]
        p = jnp.where(i < npt, pp[...], ps[...])
        t = _dot(p, w_ref[r0:r0 + kw, :])
        acc = t if acc is None else acc + t
        r0 += kw
    mod = mod_ref[...]
    x1 = x + mod[:, 2 * d:3 * d] * acc
    x1_ref[...] = x1
    h2 = _rms_mod(x1, nw_ref[...], mod[:, 3 * d:4 * d], mod[:, 4 * d:5 * d])
    h2_ref[...] = h2
    logits = _dot(h2.astype(BF16), r_ref[...])
    lane = lax.broadcasted_iota(jnp.int32, logits.shape, 1)
    logits = jnp.where(lane < N_EXPERTS, logits, -jnp.inf)
    e = jnp.exp(logits - logits.max(axis=-1, keepdims=True))
    aff = e / e.sum(axis=-1, keepdims=True)
    aff_ref[...] = aff.T[0:N_EXPERTS, :]


def _outproj(parts, x_args, n_p, seq_s, w_out, mod, nw, router):
    n = parts[0][0].shape[0] + parts[0][1].shape[0]
    tm = ROW_TILE
    npt, tps = n_p // tm, seq_s // tm
    d = D_MODEL
    dual = len(x_args) == 2
    cond_spec = pl.BlockSpec((None, 1, ADA_CHUNKS * d), lambda i: (_cond_index(i, npt, tps), 0, 0))
    pair_specs = lambda width: [pl.BlockSpec((tm, width), lambda i: (jnp.minimum(i, npt - 1), 0)),
                                pl.BlockSpec((tm, width), lambda i: (jnp.maximum(i - npt, 0), 0))]
    x_specs = pair_specs(d) if dual else [_row_spec(tm, d)]
    part_specs = [s for p in parts for s in pair_specs(p[0].shape[1])]
    parts = [a for p in parts for a in p]
    return pl.pallas_call(
        functools.partial(_outproj_kernel, npt, len(parts) // 2, dual),
        grid=(n // tm,),
        in_specs=part_specs + x_specs + [
            _const_spec(w_out.shape), cond_spec, _const_spec((1, d)), _const_spec(router.shape)],
        out_specs=[_row_spec(tm, d), _row_spec(tm, d), pl.BlockSpec((N_EXPERTS, tm), lambda i: (0, i))],
        out_shape=[jax.ShapeDtypeStruct((n, d), F32), jax.ShapeDtypeStruct((n, d), F32),
                   jax.ShapeDtypeStruct((N_EXPERTS, n), F32)],
        name="outproj",
        compiler_params=_params("parallel"),
    )(*parts, *x_args, w_out, mod, nw, router)


def _ffn_kernel(xe_ref, gate_ref, wg_ref, wu_ref, wd_ref, o_ref):
    x = xe_ref[...].astype(BF16)
    a = _dot(x, wg_ref[...])
    hid = ((a * _sigmoid(a)) * _dot(x, wu_ref[...])).astype(BF16)
    o_ref[...] = _dot(hid, wd_ref[...]) * gate_ref[...]


def _ffn(xe, gate, wg, wu, wd):
    n_e, cap, d = xe.shape
    ff = wg.shape[2]
    tm = FFN_TILE
    return pl.pallas_call(
        _ffn_kernel,
        grid=(n_e, cap // tm),
        in_specs=[pl.BlockSpec((None, tm, d), lambda e, t: (e, t, 0)),
                  pl.BlockSpec((None, tm, 1), lambda e, t: (e, t, 0)),
                  pl.BlockSpec((None, d, ff), lambda e, t: (e, 0, 0)),
                  pl.BlockSpec((None, d, ff), lambda e, t: (e, 0, 0)),
                  pl.BlockSpec((None, ff, d), lambda e, t: (e, 0, 0))],
        out_specs=pl.BlockSpec((None, tm, d), lambda e, t: (e, t, 0)),
        out_shape=jax.ShapeDtypeStruct((n_e, cap, d), F32),
        name="expert_ffn",
        compiler_params=_params("parallel", "arbitrary"),
    )(xe, gate, wg, wu, wd)


def _gather_kernel(idx_ref, src_hbm, dst_hbm, sems):
    e = pl.program_id(0)
    rows = idx_ref.shape[1]
    n_chunks = rows // GATHER_CHUNK

    def wait_chunk(c):
        window = dst_hbm.at[e, pl.ds(c * GATHER_CHUNK, GATHER_CHUNK)]
        pltpu.make_async_copy(window, window, sems.at[c % 2]).wait()

    for c in range(n_chunks):
        def issue(r, carry, c=c):
            t = idx_ref[e, r]
            pltpu.make_async_copy(src_hbm.at[pl.ds(t, 1)], dst_hbm.at[e, pl.ds(r, 1)], sems.at[c % 2]).start()
            return carry

        lax.fori_loop(c * GATHER_CHUNK, (c + 1) * GATHER_CHUNK, issue, 0, unroll=8)
        if c > 0:
            wait_chunk(c - 1)
    wait_chunk(n_chunks - 1)


def _gather_rows(idx, src):
    n_e, rows = idx.shape
    d = src.shape[1]
    return pl.pallas_call(
        _gather_kernel,
        grid_spec=pltpu.PrefetchScalarGridSpec(
            num_scalar_prefetch=1,
            grid=(n_e,),
            in_specs=[pl.BlockSpec(memory_space=pl.ANY)],
            out_specs=pl.BlockSpec(memory_space=pl.ANY),
            scratch_shapes=[pltpu.SemaphoreType.DMA((2,))]),
        out_shape=jax.ShapeDtypeStruct((n_e, rows, d), src.dtype),
        name="gather_rows",
        compiler_params=_params("arbitrary"),
    )(idx, src)


def _route(aff_t, n_p):
    n_e, n = aff_t.shape
    idxs = []
    for r0, r1 in ((0, n_p), (n_p, n)):
        cap = EC_FACTOR * (r1 - r0) // N_EXPERTS
        _, idx = lax.top_k(aff_t[:, r0:r1], cap)
        idxs.append(jnp.sort(idx, axis=1) + r0)
    idx = jnp.concatenate(idxs, axis=1)
    gate = jnp.take_along_axis(aff_t, idx, axis=1)
    mask = jnp.zeros((n_e, n), jnp.int32).at[jnp.arange(n_e)[:, None], idx].set(1)
    krank = jnp.take_along_axis(jnp.cumsum(mask, axis=0) - mask, idx, axis=1)
    cnt = mask.sum(axis=0)
    nb = n // COMBINE_TILE
    per_block = mask.reshape(n_e, nb, COMBINE_TILE).sum(axis=2)
    lo = jnp.concatenate([jnp.zeros((n_e, 1), jnp.int32), jnp.cumsum(per_block, axis=1)], axis=1)
    kmax = cnt.reshape(nb, COMBINE_TILE).max(axis=1)
    return idx, gate, krank, lo, cnt.reshape(n, 1), kmax


def _combine_kernel(nb, npb, split, idx_ref, kr_ref, lo_ref, kmax_ref, ye_hbm, cnt_ref, x1_ref, mod_ref, *refs):
    if split:
        yp_ref, ys_ref, stage, sems = refs
    else:
        y_ref, stage, sems = refs
    b = pl.program_id(0)
    t = COMBINE_TILE
    rows = idx_ref.shape[1]
    d = D_MODEL

    def issue(blk, slot):
        for e in range(N_EXPERTS):
            def body(r, carry, e=e):
                dst = kr_ref[e, r] * t + (idx_ref[e, r] - blk * t)
                pltpu.make_async_copy(ye_hbm.at[pl.ds(e * rows + r, 1)], stage.at[slot, pl.ds(dst, 1)],
                                      sems.at[slot]).start()
                return carry

            lax.fori_loop(lo_ref[e, blk], lo_ref[e, blk + 1], body, 0)

    @pl.when(b == 0)
    def _():
        issue(0, 0)

    @pl.when(b + 1 < nb)
    def _():
        issue(b + 1, (b + 1) % 2)

    slot = b % 2
    n_rows = lo_ref[0, b + 1] - lo_ref[0, b]
    for e in range(1, N_EXPERTS):
        n_rows = n_rows + (lo_ref[e, b + 1] - lo_ref[e, b])

    n_bulk = pl.multiple_of((n_rows // V7X_SUBLANES) * V7X_SUBLANES, V7X_SUBLANES)

    @pl.when(n_bulk > 0)
    def _():
        pltpu.make_async_copy(ye_hbm.at[pl.ds(0, n_bulk)], stage.at[slot, pl.ds(0, n_bulk)], sems.at[slot]).wait()

    def wait_row(j, carry):
        pltpu.make_async_copy(ye_hbm.at[pl.ds(0, 1)], stage.at[slot, pl.ds(0, 1)], sems.at[slot]).wait()
        return carry

    lax.fori_loop(0, n_rows - n_bulk, wait_row, 0)

    cnt = cnt_ref[...]

    def plane(k, acc):
        k0 = pl.multiple_of(k * t, t)
        return acc + jnp.where(cnt > k, stage[slot, pl.ds(k0, t), :], 0.0)

    acc = lax.fori_loop(0, kmax_ref[b], plane, jnp.zeros((t, d), F32))
    y = x1_ref[...] + mod_ref[:, 5 * d:6 * d] * acc
    if split:
        @pl.when(b < npb)
        def _():
            yp_ref[...] = y

        @pl.when(b >= npb)
        def _():
            ys_ref[...] = y
    else:
        y_ref[...] = y


def _combine(tables, ye, x1, n_p, seq_s, mod, split):
    idx, krank, lo, cnt, kmax = tables
    n, d = x1.shape
    t = COMBINE_TILE
    nb, npb, bps = n // t, n_p // t, seq_s // t
    row_spec = pl.BlockSpec((t, d), lambda i, *_: (i, 0))
    if split:
        out_specs = [pl.BlockSpec((t, d), lambda i, *_: (jnp.minimum(i, npb - 1), 0)),
                     pl.BlockSpec((t, d), lambda i, *_: (jnp.maximum(i - npb, 0), 0))]
        out_shape = [jax.ShapeDtypeStruct((n_p, d), F32), jax.ShapeDtypeStruct((n - n_p, d), F32)]
    else:
        out_specs = row_spec
        out_shape = jax.ShapeDtypeStruct((n, d), F32)
    return pl.pallas_call(
        functools.partial(_combine_kernel, nb, npb, split),
        grid_spec=pltpu.PrefetchScalarGridSpec(
            num_scalar_prefetch=4,
            grid=(nb,),
            in_specs=[pl.BlockSpec(memory_space=pl.ANY),
                      pl.BlockSpec((t, 1), lambda i, *_: (i, 0)),
                      row_spec,
                      pl.BlockSpec((None, 1, ADA_CHUNKS * d), lambda i, *_: (_cond_index(i, npb, bps), 0, 0))],
            out_specs=out_specs,
            scratch_shapes=[pltpu.VMEM((2, N_EXPERTS * t, d), F32), pltpu.SemaphoreType.DMA((2,))]),
        out_shape=out_shape,
        name="moe_combine",
        compiler_params=_params("arbitrary"),
    )(idx, krank, lo, kmax, ye.reshape(-1, d), cnt, x1, mod)


def _moe(h2, aff_t, x1, n_p, seq_s, mod, wg, wu, wd, split):
    idx, gate, krank, lo, cnt, kmax = _route(aff_t, n_p)
    xe = _gather_rows(idx, h2)
    ye = _ffn(xe, gate[..., None], wg, wu, wd)
    return _combine((idx, krank, lo, cnt, kmax), ye, x1, n_p, seq_s, mod, split)


def kernel(x_prompt, x_sample, cache_l0_k, cache_l0_v, state_l0_fwd, state_l0_bwd, cache_l1_k, cache_l1_v, c, c_ctx, l0_norm1, l0_norm2, l0_ada_w, l0_ada_b, l0_w_in, l0_q_gain, l0_k_gain, l0_conv_w, l0_conv_b, l0_fwd_wa, l0_fwd_ba, l0_fwd_wi, l0_fwd_bi, l0_fwd_lam, l0_bwd_wa, l0_bwd_ba, l0_bwd_wi, l0_bwd_bi, l0_bwd_lam, l0_w_out, l0_router, l0_w_gate, l0_w_up, l0_w_down, l1_norm1, l1_norm2, l1_ada_w, l1_ada_b, l1_w_in, l1_q_gain, l1_k_gain, l1_lam_q1, l1_lam_k1, l1_lam_q2, l1_lam_k2, l1_subln, l1_w_out, l1_router, l1_w_gate, l1_w_up, l1_w_down):
    d = D_MODEL
    bp, seq_p, _ = x_prompt.shape
    bs, seq_s, _ = x_sample.shape
    n_p, n_s = bp * seq_p, bs * seq_s
    xp = x_prompt.reshape(n_p, d)
    xs = x_sample.reshape(n_s, d)

    cond = jnp.zeros((COND_ROWS, d), F32).at[0].set(c_ctx).at[1:1 + bs].set(c)
    mod0 = _adaln(cond, l0_ada_w, l0_ada_b).reshape(COND_ROWS, 1, ADA_CHUNKS * d)
    mod1 = _adaln(cond, l1_ada_w, l1_ada_b).reshape(COND_ROWS, 1, ADA_CHUNKS * d)
    gmat = _group_matrix()
    cos, sin = _rope_tables(seq_s, ROW_TILE)
    row = lambda v: v.reshape(1, -1)
    tile_gain = lambda g, width: jnp.tile(g, width // HEAD_DIM).reshape(1, width)
    pad_router = lambda r: jnp.pad(r, ((0, 0), (0, ROUTER_PAD - N_EXPERTS))).astype(BF16)

    q, k, v, xb, gb, k0, v0 = _proj0(xp, xs, seq_s, mod0, row(l0_norm1), l0_w_in.astype(BF16), gmat,
                                     tile_gain(l0_q_gain, A_Q_W), tile_gain(l0_k_gain, A_KV_W), cos, sin)
    attn = _attention("gqa", _gqa_kernel, [], q, k, v, cache_l0_k.reshape(-1, A_KV_W), cache_l0_v.reshape(-1, A_KV_W),
                      n_p, seq_p, seq_s, A_Q_W)
    cw = jnp.concatenate([l0_conv_w, l0_conv_b[None], jnp.zeros((3, LRU_WIDTH), F32)], axis=0)
    wg = jnp.stack([_lru_gate_weights(l0_fwd_wa, l0_fwd_wi), _lru_gate_weights(l0_bwd_wa, l0_bwd_wi)])
    pad5 = jnp.zeros((5, LRU_WIDTH), F32)
    vec = jnp.stack([jnp.concatenate([l0_fwd_ba[None], l0_fwd_bi[None], l0_fwd_lam[None], pad5], axis=0),
                     jnp.concatenate([l0_bwd_ba[None], l0_bwd_bi[None], l0_bwd_lam[None], pad5], axis=0)])
    rec_p, st = _lru_call(xb, gb, 0, bp, seq_p, cw, wg, vec, jnp.zeros((bp, 2, LRU_WIDTH), F32))
    rec_s, _ = _lru_call(xb, gb, n_p, bs, seq_s, cw, wg, vec, jnp.stack([state_l0_fwd, state_l0_bwd], axis=1))
    x1, h2, aff_t = _outproj([attn, (rec_p, rec_s)], [xp, xs], n_p, seq_s, l0_w_out.astype(BF16), mod0, row(l0_norm2),
                             pad_router(l0_router))
    x2 = _moe(h2, aff_t, x1, n_p, seq_s, mod0,
              l0_w_gate.astype(BF16), l0_w_up.astype(BF16), l0_w_down.astype(BF16), False)

    q, k, v, k1, v1 = _proj1(x2, n_p, seq_s, mod1, row(l1_norm1), l1_w_in.astype(BF16), gmat,
                             tile_gain(l1_q_gain, C_QK_W), tile_gain(l1_k_gain, C_QK_W), cos, sin)
    lam_p = jnp.concatenate([l1_lam_q1[None], l1_lam_k1[None], l1_lam_q2[None], l1_lam_k2[None],
                             jnp.zeros((4, HEAD_DIM), F32)], axis=0)
    o = _attention("diff", _diff_kernel, [lam_p, row(l1_subln)], q, k, v, cache_l1_k.reshape(-1, C_QK_W),
                   cache_l1_v.reshape(-1, C_QK_W), n_p, seq_p, seq_s, C_QK_W)
    x1, h2, aff_t = _outproj([o], [x2], n_p, seq_s, l1_w_out.astype(BF16), mod1, row(l1_norm2),
                             pad_router(l1_router))
    y_p, y_s = _moe(h2, aff_t, x1, n_p, seq_s, mod1,
                    l1_w_gate.astype(BF16), l1_w_up.astype(BF16), l1_w_down.astype(BF16), True)

    return (y_p.reshape(bp, seq_p, d), y_s.reshape(bs, seq_s, d),
            k0.reshape(bp, seq_p, A_KV_HEADS, HEAD_DIM), v0.reshape(bp, seq_p, A_KV_HEADS, HEAD_DIM),
            st[:, 0], st[:, 1],
            k1.reshape(bp, seq_p, 2, C_HEADS, HEAD_DIM), v1.reshape(bp, seq_p, C_HEADS, 2 * HEAD_DIM))
```
